```python
import functools
import jax, jax.numpy as jnp
from jax import lax
import numpy as np

D_MODEL = 1024
BATCH = 4
SEQ = 8192
DEPTH = 1
DEC_BATCH = 128
DEC_SEQ = 1
PAST_LEN = 16384
PAGE_SIZE = 128

N_HEADS = 8
D_NOPE = 64
D_ROPE = 32
D_V = 64
Q_RANK = 384
KV_RANK = 256
ROPE_THETA = 10000.0
SM_SCALE = (D_NOPE + D_ROPE) ** -0.5
Q_BLOCK = 128
D_POOL = 512
POOL_WINDOWS = (2, 4, 8, 16)
N_POOL_GROUPS = len(POOL_WINDOWS)
POOL_GROUP = D_POOL // N_POOL_GROUPS
POOL_OUT_GROUP = D_MODEL // N_POOL_GROUPS
POOL_STATE = max(POOL_WINDOWS) - 1
D_FF = ((8 * D_MODEL // 3 + 255) // 256) * 256
IN_COLS = Q_RANK + KV_RANK + D_ROPE + D_POOL + 2 * D_MODEL
EPS = 1e-6

kernel_name = "mla_pool_parallel_gated_adaln_step"


def _rmsnorm(x, g):
    x32 = x.astype(jnp.float32)
    y = x32 * lax.rsqrt(jnp.mean(x32 * x32, axis=-1, keepdims=True) + EPS)
    return y.astype(x.dtype) * g


def _rope_cos_sin(pos):
    inv = 1.0 / (ROPE_THETA ** (jnp.arange(0, D_ROPE, 2, dtype=jnp.float32) / D_ROPE))
    ang = pos.astype(jnp.float32)[:, None] * inv[None, :]
    return jnp.cos(ang), jnp.sin(ang)


def _rope(x, cos, sin):
    x32 = x.astype(jnp.float32)
    x1, x2 = jnp.split(x32, 2, axis=-1)
    return jnp.concatenate([x1 * cos - x2 * sin, x2 * cos + x1 * sin], axis=-1).astype(x.dtype)


def _prompt_mla(q_nope, q_pe, c_kv, k_pe, w_uk, g_k_nope, w_uv):
    b, s = q_nope.shape[:2]
    nb = s // Q_BLOCK
    k_nope = _rmsnorm(jnp.einsum("bsr,rhd->bshd", c_kv, w_uk), g_k_nope)
    v = jnp.einsum("bsr,rhd->bshd", c_kv, w_uv)
    k_pos = jnp.arange(s)

    def block(args):
        qn, qp, start = args
        sc = (jnp.einsum("bqhd,bkhd->bhqk", qn, k_nope, preferred_element_type=jnp.float32)
              + jnp.einsum("bqhd,bkd->bhqk", qp, k_pe, preferred_element_type=jnp.float32)) * SM_SCALE
        q_pos = start + jnp.arange(Q_BLOCK)
        sc = jnp.where(k_pos[None, :] <= q_pos[:, None], sc, -jnp.inf)
        p = jax.nn.softmax(sc, axis=-1).astype(v.dtype)
        return jnp.einsum("bhqk,bkhd->bqhd", p, v)

    def to_blocks(t):
        return jnp.moveaxis(t.reshape(b, nb, Q_BLOCK, *t.shape[2:]), 1, 0)

    o = lax.map(block, (to_blocks(q_nope), to_blocks(q_pe), jnp.arange(nb) * Q_BLOCK))
    return jnp.moveaxis(o, 0, 1).reshape(b, s, N_HEADS, D_V)


def _sample_mla(q_nope, q_pe, c_kv, k_pe, cache_kv_latent, cache_k_rope, page_table, layer, w_uk, g_k_nope, w_uv):
    t = q_nope.shape[1]

    def one(args):
        pt, qn, qp, cn, kn = args
        lat = jnp.concatenate([cache_kv_latent[layer, pt].reshape(-1, KV_RANK), cn], axis=0)
        kpe = jnp.concatenate([cache_k_rope[layer, pt].reshape(-1, D_ROPE), kn], axis=0)
        past = lat.shape[0] - t
        k_nope = _rmsnorm(jnp.einsum("lr,rhd->lhd", lat, w_uk), g_k_nope)
        sc = (jnp.einsum("thd,lhd->htl", qn, k_nope, preferred_element_type=jnp.float32)
              + jnp.einsum("thd,ld->htl", qp, kpe, preferred_element_type=jnp.float32)) * SM_SCALE
        mask = jnp.arange(past + t)[None, :] <= past + jnp.arange(t)[:, None]
        p = jax.nn.softmax(jnp.where(mask, sc, -jnp.inf), axis=-1).astype(lat.dtype)
        o_lat = jnp.einsum("htl,lr->thr", p, lat)
        return jnp.einsum("thr,rhd->thd", o_lat, w_uv)

    return lax.map(one, (page_table, q_nope, q_pe, c_kv, k_pe))


def _multi_scale_pool(u_ext, pos0, n_out):
    b, l, _ = u_ext.shape
    u32 = u_ext.astype(jnp.float32)
    csz = jnp.concatenate([jnp.zeros((b, 1, D_POOL), jnp.float32), jnp.cumsum(u32, axis=1)], axis=1)
    i = jnp.arange(l - n_out, l)
    abs_pos = pos0 + i
    outs = []
    for g, w in enumerate(POOL_WINDOWS):
        cs = csz[:, :, g * POOL_GROUP:(g + 1) * POOL_GROUP]
        lo = jnp.maximum(i + 1 - w, 0)
        cnt = jnp.minimum(abs_pos + 1, w).astype(jnp.float32)
        outs.append((cs[:, i + 1] - cs[:, lo]) / cnt[None, :, None])
    mean = jnp.concatenate(outs, axis=-1)
    return (mean - u32[:, l - n_out:]).astype(u_ext.dtype)


def _layer(x, c, pos, attend, pool_prev, pool_pos0, w_ada, b_ada, g_norm1, w_in, g_q_lat, w_uq, g_kv_lat,
           g_q_nope, g_q_rope, g_k_rope, w_attn_o, w_pool, s_pool, w_out, g_norm2, w_gu, w_down):
    b, s, _ = x.shape
    mod = jax.nn.silu(c) @ w_ada + b_ada
    sh1, sc1, gt1, sh2, sc2, gt2 = jnp.split(mod[:, None, :], 6, axis=-1)
    h = _rmsnorm(x, g_norm1) * (1 + sc1) + sh1
    proj = h @ w_in
    idx = np.cumsum([Q_RANK, KV_RANK, D_ROPE, D_POOL, D_MODEL]).tolist()
    q_lat, c_kv, k_pe, u, gate_a, gate_b = jnp.split(proj, idx, axis=-1)
    q = jnp.einsum("bsr,rhd->bshd", _rmsnorm(q_lat, g_q_lat), w_uq)
    cos, sin = _rope_cos_sin(pos)
    q_nope = _rmsnorm(q[..., :D_NOPE], g_q_nope)
    q_pe = _rope(_rmsnorm(q[..., D_NOPE:], g_q_rope), cos[:, None, :], sin[:, None, :])
    c_kv = _rmsnorm(c_kv, g_kv_lat)
    k_pe = _rope(_rmsnorm(k_pe, g_k_rope), cos, sin)
    o = attend(q_nope, q_pe, c_kv, k_pe)
    a = jnp.einsum("bshd,hde->bse", o, w_attn_o)
    u_ext = jnp.concatenate([pool_prev, u], axis=1)
    d = _multi_scale_pool(u_ext, pool_pos0, s)
    bb = jnp.einsum("bsgc,gce->bsge", d.reshape(b, s, N_POOL_GROUPS, POOL_GROUP), w_pool).reshape(b, s, D_MODEL) * s_pool
    m = jax.nn.sigmoid(gate_a) * a + jax.nn.sigmoid(gate_b) * bb
    x = x + gt1 * (m @ w_out)
    h2 = _rmsnorm(x, g_norm2) * (1 + sc2) + sh2
    gg, up = jnp.split(h2 @ w_gu, 2, axis=-1)
    x = x + gt2 * ((jax.nn.silu(gg) * up) @ w_down)
    return x, c_kv, k_pe, u_ext[:, -POOL_STATE:]


def setup_inputs(seed: int = 0) -> dict:
    key = jax.random.key(seed)
    ks = iter(jax.random.split(key, 40))
    f32 = jnp.float32

    def nrm(shape, scale):
        return jax.random.normal(next(ks), shape, f32) * scale

    def gain(n):
        return 1.0 + 0.05 * jax.random.normal(next(ks), (DEPTH, n), f32)

    n_pages = PAST_LEN // PAGE_SIZE
    n_phys = (5 * DEC_BATCH * n_pages) // 4
    perm = jax.random.permutation(next(ks), n_phys)[: DEC_BATCH * n_pages]
    page_table = perm.reshape(DEC_BATCH, n_pages).astype(jnp.int32)
    return {
        "x_prompt": nrm((BATCH, SEQ, D_MODEL), 1.0),
        "x_sample": nrm((DEC_BATCH, DEC_SEQ, D_MODEL), 1.0),
        "cache_kv_latent": nrm((DEPTH, n_phys, PAGE_SIZE, KV_RANK), 1.0),
        "cache_k_rope": nrm((DEPTH, n_phys, PAGE_SIZE, D_ROPE), 1.0),
        "state_pool": nrm((DEPTH, DEC_BATCH, POOL_STATE, D_POOL), 1.0),
        "page_table": page_table,
        "c_prompt": nrm((BATCH, D_MODEL), 1.0),
        "c_sample": nrm((DEC_BATCH, D_MODEL), 1.0),
        "w_ada": nrm((DEPTH, D_MODEL, 6 * D_MODEL), D_MODEL ** -0.5),
        "b_ada": nrm((DEPTH, 6 * D_MODEL), 0.02),
        "g_norm1": gain(D_MODEL),
        "w_in": nrm((DEPTH, D_MODEL, IN_COLS), D_MODEL ** -0.5),
        "g_q_lat": gain(Q_RANK),
        "w_uq": nrm((DEPTH, Q_RANK, N_HEADS, D_NOPE + D_ROPE), Q_RANK ** -0.5),
        "g_kv_lat": gain(KV_RANK),
        "g_q_nope": gain(D_NOPE),
        "g_q_rope": gain(D_ROPE),
        "g_k_nope": gain(D_NOPE),
        "g_k_rope": gain(D_ROPE),
        "w_uk": nrm((DEPTH, KV_RANK, N_HEADS, D_NOPE), KV_RANK ** -0.5),
        "w_uv": nrm((DEPTH, KV_RANK, N_HEADS, D_V), KV_RANK ** -0.5),
        "w_attn_o": nrm((DEPTH, N_HEADS, D_V, D_MODEL), (N_HEADS * D_V) ** -0.5),
        "w_pool": nrm((DEPTH, N_POOL_GROUPS, POOL_GROUP, POOL_OUT_GROUP), POOL_GROUP ** -0.5),
        "s_pool": gain(D_MODEL),
        "w_out": nrm((DEPTH, D_MODEL, D_MODEL), D_MODEL ** -0.5),
        "g_norm2": gain(D_MODEL),
        "w_gu": nrm((DEPTH, D_MODEL, 2 * D_FF), D_MODEL ** -0.5),
        "w_down": nrm((DEPTH, D_FF, D_MODEL), D_FF ** -0.5),
    }


def reference(x_prompt, x_sample, cache_kv_latent, cache_k_rope, state_pool, page_table, c_prompt, c_sample,
              w_ada, b_ada, g_norm1, w_in, g_q_lat, w_uq, g_kv_lat, g_q_nope, g_q_rope, g_k_nope, g_k_rope,
              w_uk, w_uv, w_attn_o, w_pool, s_pool, w_out, g_norm2, w_gu, w_down):
    pos_p = jnp.arange(x_prompt.shape[1])
    pos_s = PAST_LEN + jnp.arange(x_sample.shape[1])
    yp, ys = x_prompt, x_sample
    lat_p, kr_p, pool_p, lat_s, kr_s, pool_s = [], [], [], [], [], []
    for l in range(DEPTH):
        shared = (w_ada[l], b_ada[l], g_norm1[l], w_in[l], g_q_lat[l], w_uq[l], g_kv_lat[l], g_q_nope[l],
                  g_q_rope[l], g_k_rope[l], w_attn_o[l], w_pool[l], s_pool[l], w_out[l], g_norm2[l],
                  w_gu[l], w_down[l])
        attend_p = functools.partial(_prompt_mla, w_uk=w_uk[l], g_k_nope=g_k_nope[l], w_uv=w_uv[l])
        attend_s = functools.partial(_sample_mla, cache_kv_latent=cache_kv_latent, cache_k_rope=cache_k_rope,
                                     page_table=page_table, layer=l, w_uk=w_uk[l], g_k_nope=g_k_nope[l],
                                     w_uv=w_uv[l])
        empty = jnp.zeros((yp.shape[0], 0, D_POOL), yp.dtype)
        yp, lp, kp, pp = _layer(yp, c_prompt, pos_p, attend_p, empty, 0, *shared)
        ys, lsm, ksm, psm = _layer(ys, c_sample, pos_s, attend_s, state_pool[l], PAST_LEN - POOL_STATE, *shared)
        lat_p.append(lp); kr_p.append(kp); pool_p.append(pp)
        lat_s.append(lsm); kr_s.append(ksm); pool_s.append(psm)
    return (yp, ys, jnp.stack(lat_p), jnp.stack(kr_p), jnp.stack(pool_p),
            jnp.stack(lat_s), jnp.stack(kr_s), jnp.stack(pool_s))
```

```python
import functools
import math

import jax
import jax.numpy as jnp
import numpy as np
from jax import lax
from jax.experimental import pallas as pl
from jax.experimental.pallas import tpu as pltpu

N_HEADS = 8
D_NOPE = 64
D_ROPE = 32
D_V = 64
ROPE_THETA = 10000.0
SM_SCALE = (D_NOPE + D_ROPE) ** -0.5
POOL_WINDOWS = (2, 4, 8, 16)
POOL_STATE = max(POOL_WINDOWS) - 1
EPS = 1e-6

LANES = 128
SUBLANES = 8
HEAD_BLOCK = LANES
VMEM_LIMIT_BYTES = 56 * 1024 * 1024

BF16 = jnp.bfloat16
F32 = jnp.float32
NEG_BIG = -1e30


def _dot(a, b):
    return jnp.dot(a, b, preferred_element_type=F32)


def _dot_nt(a, b):
    return lax.dot_general(a, b, (((1,), (1,)), ((), ())), preferred_element_type=F32)


def _rms(x):
    return x * lax.rsqrt(jnp.mean(x * x, axis=-1, keepdims=True) + EPS)


def _cparams(sem):
    return pltpu.CompilerParams(dimension_semantics=sem, vmem_limit_bytes=VMEM_LIMIT_BYTES)


def _const_spec(shape):
    nd = len(shape)
    return pl.BlockSpec(shape, lambda *_: (0,) * nd, pipeline_mode=pl.Buffered(1))


def _adaln_kernel(c_ref, w_ref, b_ref, o_ref):
    c = c_ref[...]
    s = (c * jax.nn.sigmoid(c)).astype(BF16)
    o_ref[...] = _dot(s, w_ref[...]) + b_ref[...]


def _adaln(c, w_bf, b):
    rows, d = c.shape
    n = w_bf.shape[1]
    bn = 1024
    return pl.pallas_call(
        _adaln_kernel,
        out_shape=jax.ShapeDtypeStruct((rows, n), F32),
        grid=(n // bn,),
        in_specs=[pl.BlockSpec((rows, d), lambda j: (0, 0)),
                  pl.BlockSpec((d, bn), lambda j: (0, j)),
                  pl.BlockSpec((1, bn), lambda j: (0, j))],
        out_specs=pl.BlockSpec((rows, bn), lambda j: (0, j)),
        compiler_params=_cparams(("arbitrary",)),
        name="adaln",
    )(c, w_bf, b)


def _inproj_kernel(x_ref, sh_ref, sc_ref, cos_ref, sin_ref, g1_ref, win_ref, gql_ref, wuq_ref, wuqs_ref,
                   gmat_ref, gq_ref, gqs_ref, gkv_ref, wuk_ref, gkmat_ref, gk_ref, wuv_ref, gkr_ref, gkrs_ref,
                   q_ref, k_ref, v_ref, lat_ref, kpe_ref, u_ref, sa_ref, sb_ref, *, q_rank, kv_rank, d_pool, d_model):
    x = x_ref[...]
    h = _rms(x) * g1_ref[...] * (1.0 + sc_ref[...]) + sh_ref[...]
    hb = h.astype(BF16)

    c0 = 0
    c1 = c0 + q_rank
    c2 = c1 + kv_rank
    c3 = c2 + d_pool
    c4 = c3 + d_model
    c5 = c4 + d_model
    c6 = c5 + HEAD_BLOCK
    c7 = c6 + HEAD_BLOCK

    cos8 = jnp.concatenate([cos_ref[...]] * N_HEADS, axis=1)
    sin8 = jnp.concatenate([sin_ref[...]] * N_HEADS, axis=1)

    qn = (_rms(_dot(hb, win_ref[:, c0:c1])) * gql_ref[...]).astype(BF16)
    q_raw = _dot(qn, wuq_ref[...])
    q_swp = _dot(qn, wuqs_ref[...])
    r = lax.rsqrt(_dot((q_raw * q_raw).astype(BF16), gmat_ref[...]) + EPS)
    q = (q_raw * r * gq_ref[...]) * cos8 + (q_swp * r * gqs_ref[...]) * sin8
    q_ref[...] = q.astype(BF16)

    ckv = _rms(_dot(hb, win_ref[:, c1:c2])) * gkv_ref[...]
    lat_ref[...] = ckv
    cb = ckv.astype(BF16)

    kpe = _dot(hb, win_ref[:, c5:c6])
    kpe_s = _dot(hb, win_ref[:, c6:c7])
    rk = lax.rsqrt(jnp.sum(kpe * kpe, axis=-1, keepdims=True) * (1.0 / D_ROPE) + EPS)
    kpe_rot = (kpe * rk * gkr_ref[...]) * cos_ref[...] + (kpe_s * rk * gkrs_ref[...]) * sin_ref[...]
    kpe_ref[...] = kpe_rot

    kn_raw = _dot(cb, wuk_ref[...])
    rn = lax.rsqrt(_dot((kn_raw * kn_raw).astype(BF16), gkmat_ref[...]) + EPS)
    kfull = kn_raw * rn * gk_ref[...] + jnp.concatenate([kpe_rot] * N_HEADS, axis=1)
    k_ref[...] = kfull.astype(BF16)
    v_ref[...] = _dot(cb, wuv_ref[...]).astype(BF16)

    u_ref[...] = _dot(hb, win_ref[:, c2:c3])
    sa_ref[...] = jax.nn.sigmoid(_dot(hb, win_ref[:, c3:c4])).astype(BF16)
    sb_ref[...] = jax.nn.sigmoid(_dot(hb, win_ref[:, c4:c5])).astype(BF16)


def _inproj(x, sh, sc, cos_t, sin_t, w, *, rows_per_mod, tm):
    n, d = x.shape
    q_rank, kv_rank, d_pool = w["q_rank"], w["kv_rank"], w["d_pool"]
    hw = N_HEADS * HEAD_BLOCK
    if rows_per_mod == 1:
        mod_spec = pl.BlockSpec((tm, d), lambda i: (i, 0))
        sh = sh.reshape(n, d)
        sc = sc.reshape(n, d)
    else:
        tiles_per_mod = rows_per_mod // tm
        mod_spec = pl.BlockSpec((None, 1, d), lambda i: (i // tiles_per_mod, 0, 0))
    row = lambda width: pl.BlockSpec((tm, width), lambda i: (i, 0))
    consts = [w["g_norm1"], w["w_in"], w["g_q_lat"], w["w_uq"], w["w_uq_swap"], w["gmat_q"], w["gq"], w["gq_swap"],
              w["g_kv_lat"], w["w_uk"], w["gmat_k"], w["gk"], w["w_uv"], w["gkr"], w["gkr_swap"]]
    out_shapes = (jax.ShapeDtypeStruct((n, hw), BF16), jax.ShapeDtypeStruct((n, hw), BF16),
                  jax.ShapeDtypeStruct((n, hw), BF16), jax.ShapeDtypeStruct((n, kv_rank), F32),
                  jax.ShapeDtypeStruct((n, HEAD_BLOCK), F32), jax.ShapeDtypeStruct((n, d_pool), F32),
                  jax.ShapeDtypeStruct((n, d), BF16), jax.ShapeDtypeStruct((n, d), BF16))
    return pl.pallas_call(
        functools.partial(_inproj_kernel, q_rank=q_rank, kv_rank=kv_rank, d_pool=d_pool, d_model=d),
        out_shape=out_shapes,
        grid=(n // tm,),
        in_specs=[row(d), mod_spec, mod_spec, row(HEAD_BLOCK), row(HEAD_BLOCK)] + [_const_spec(c.shape) for c in consts],
        out_specs=(row(hw), row(hw), row(hw), row(kv_rank), row(HEAD_BLOCK), row(d_pool), row(d), row(d)),
        compiler_params=_cparams(("arbitrary",)),
        name="inproj",
    )(x, sh, sc, cos_t, sin_t, *consts)


def _attn_kernel(q_ref, k_ref, v_ref, o_ref, m_sc, l_sc, acc_sc, *, tq):
    qi = pl.program_id(2)
    q = q_ref[...]
    m_sc[...] = jnp.full(m_sc.shape, NEG_BIG, F32)
    l_sc[...] = jnp.zeros(l_sc.shape, F32)
    acc_sc[...] = jnp.zeros(acc_sc.shape, F32)

    def step(j, masked):
        start = pl.multiple_of(j * tq, tq)
        k = k_ref[pl.ds(start, tq), :]
        v = v_ref[pl.ds(start, tq), :]
        s = _dot_nt(q, k)
        if masked:
            rows = lax.broadcasted_iota(jnp.int32, s.shape, 0)
            cols = lax.broadcasted_iota(jnp.int32, s.shape, 1)
            s = jnp.where(cols <= rows, s, NEG_BIG)
        m_old = m_sc[...]
        m_new = jnp.maximum(m_old, jnp.max(s, axis=-1, keepdims=True))
        alpha = jnp.exp(m_old - m_new)
        p = jnp.exp(s - m_new)
        l_sc[...] = alpha * l_sc[...] + jnp.sum(p, axis=-1, keepdims=True)
        acc_sc[...] = alpha * acc_sc[...] + _dot(p.astype(BF16), v)
        m_sc[...] = m_new

    def body(j, carry):
        step(j, False)
        return carry

    lax.fori_loop(0, qi, body, 0)
    step(qi, True)
    o_ref[...] = (acc_sc[...] / l_sc[...]).astype(BF16)


def _attn(q, k, v, *, batch, seq, tq):
    n, hw = q.shape
    nq = seq // tq
    qspec = pl.BlockSpec((tq, HEAD_BLOCK), lambda b, h, i: (b * nq + i, h))
    kvspec = pl.BlockSpec((seq, HEAD_BLOCK), lambda b, h, i: (b, h))
    return pl.pallas_call(
        functools.partial(_attn_kernel, tq=tq),
        out_shape=jax.ShapeDtypeStruct((n, hw), BF16),
        grid=(batch, N_HEADS, nq),
        in_specs=[qspec, kvspec, kvspec],
        out_specs=qspec,
        scratch_shapes=[pltpu.VMEM((tq, 1), F32), pltpu.VMEM((tq, 1), F32), pltpu.VMEM((tq, HEAD_BLOCK), F32)],
        compiler_params=_cparams(("arbitrary", "arbitrary", "arbitrary")),
        name="attn",
    )(q, k, v)


def _dec_attn_kernel(pt_ref, q_ref, qpe_ref, cnew_ref, kpenew_ref, wukt_ref, wukq_ref, wuv_ref, lat_hbm, kpe_hbm,
                     o_ref, latbuf, kpebuf, waug, lat_sem, kpe_sem, m_sc, l_sc, acc_sc,
                     *, layer, n_chunks, pages_per_chunk, page, sub):
    b = pl.program_id(0)
    nb = pl.num_programs(0)
    kv_rank = latbuf.shape[-1]
    n_k = N_HEADS * D_NOPE
    chunk_rows = pages_per_chunk * page

    def lat_copy(pg, slot, p):
        return pltpu.make_async_copy(lat_hbm.at[layer, pg], latbuf.at[slot, pl.ds(p * page, page), :],
                                     lat_sem.at[slot])

    def kpe_copy(pg, slot, p):
        return pltpu.make_async_copy(kpe_hbm.at[layer, pg], kpebuf.at[slot, pl.ds(p * page, page), :],
                                     kpe_sem.at[slot])

    def start_chunk(seq, c, slot):
        for p in range(pages_per_chunk):
            pg = pt_ref[seq, c * pages_per_chunk + p]
            lat_copy(pg, slot, p).start()
            kpe_copy(pg, slot, p).start()

    def wait_chunk(slot):
        for p in range(pages_per_chunk):
            lat_copy(0, slot, p).wait()
            kpe_copy(0, slot, p).wait()

    @pl.when(b == 0)
    def _():
        waug[0:n_k, :] = wukt_ref[...]
        start_chunk(0, 0, 0)

    qrow = q_ref[...]
    q8 = jnp.concatenate([qrow] * N_HEADS, axis=1)
    rr = lax.broadcasted_iota(jnp.int32, q8.shape, 0)
    cc = lax.broadcasted_iota(jnp.int32, q8.shape, 1)
    qbd = jnp.where(cc // HEAD_BLOCK == rr, q8, 0.0).astype(BF16)
    qabs = _dot(qbd, wukq_ref[...])
    waug[n_k:, :] = jnp.concatenate([qabs, jnp.zeros_like(qabs)], axis=0).astype(BF16)
    qpe = qpe_ref[...].astype(BF16)

    m_sc[...] = jnp.full(m_sc.shape, NEG_BIG, F32)
    l_sc[...] = jnp.zeros(l_sc.shape, F32)
    acc_sc[...] = jnp.zeros(acc_sc.shape, F32)

    def update(latb, kpeb, n_valid):
        kt = _dot_nt(waug[...], latb)
        kraw = kt[0:n_k]
        ssq = jnp.sum((kraw * kraw).reshape(N_HEADS, D_NOPE, kraw.shape[-1]), axis=1)
        s = kt[n_k:n_k + N_HEADS] * lax.rsqrt(ssq * (1.0 / D_NOPE) + EPS) + _dot_nt(qpe, kpeb)
        if n_valid is not None:
            cols = lax.broadcasted_iota(jnp.int32, s.shape, 1)
            s = jnp.where(cols < n_valid, s, NEG_BIG)
        m_old = m_sc[...]
        m_new = jnp.maximum(m_old, jnp.max(s, axis=-1, keepdims=True))
        alpha = jnp.exp(m_old - m_new)
        p = jnp.exp(s - m_new)
        l_sc[...] = alpha * l_sc[...] + jnp.sum(p, axis=-1, keepdims=True)
        acc_sc[...] = alpha * acc_sc[...] + _dot(p.astype(BF16), latb)
        m_sc[...] = m_new

    def chunk_body(c, carry):
        g = b * n_chunks + c
        slot = jnp.bitwise_and(g, 1)

        @pl.when(c + 1 < n_chunks)
        def _():
            start_chunk(b, c + 1, 1 - slot)

        @pl.when(jnp.logical_and(c + 1 == n_chunks, b + 1 < nb))
        def _():
            start_chunk(b + 1, 0, 1 - slot)

        wait_chunk(slot)
        for t in range(chunk_rows // sub):
            latb = latbuf[slot, pl.ds(t * sub, sub), :].astype(BF16)
            kpeb = kpebuf[slot, pl.ds(t * sub, sub), :].astype(BF16)
            update(latb, kpeb, None)
        return carry

    lax.fori_loop(0, n_chunks, chunk_body, 0)

    rows = lax.broadcasted_iota(jnp.int32, (LANES, kv_rank), 0)
    lat_new = jnp.where(rows == 0, cnew_ref[...], 0.0).astype(BF16)
    rows_r = lax.broadcasted_iota(jnp.int32, (LANES, D_ROPE), 0)
    kpe_new = jnp.where(rows_r == 0, kpenew_ref[...], 0.0).astype(BF16)
    update(lat_new, kpe_new, 1)

    o_lat = (acc_sc[...] / l_sc[...]).astype(BF16)
    of = _dot(o_lat, wuv_ref[...])
    r2 = lax.broadcasted_iota(jnp.int32, of.shape, 0)
    c2 = lax.broadcasted_iota(jnp.int32, of.shape, 1)
    o_ref[...] = jnp.sum(jnp.where(c2 // D_V == r2, of, 0.0), axis=0, keepdims=True)


def _dec_attn(page_table, q3, qpe3, c_new, kpe_new, w, cache_lat, cache_kpe, *, layer):
    db, n_pages = page_table.shape
    _, _, page, kv_rank = cache_lat.shape
    pages_per_chunk = math.gcd(n_pages, 16)
    n_chunks = n_pages // pages_per_chunk
    chunk_rows = pages_per_chunk * page
    sub = math.gcd(chunk_rows, 512)
    n_k = N_HEADS * D_NOPE
    kernel = functools.partial(_dec_attn_kernel, layer=layer, n_chunks=n_chunks, pages_per_chunk=pages_per_chunk,
                               page=page, sub=sub)
    grid_spec = pltpu.PrefetchScalarGridSpec(
        num_scalar_prefetch=1,
        grid=(db,),
        in_specs=[pl.BlockSpec((None, N_HEADS, HEAD_BLOCK), lambda b, pt: (b, 0, 0)),
                  pl.BlockSpec((None, N_HEADS, D_ROPE), lambda b, pt: (b, 0, 0)),
                  pl.BlockSpec((None, 1, kv_rank), lambda b, pt: (b, 0, 0)),
                  pl.BlockSpec((None, 1, D_ROPE), lambda b, pt: (b, 0, 0)),
                  pl.BlockSpec(w["w_ukt"].shape, lambda b, pt: (0, 0)),
                  pl.BlockSpec(w["w_ukq"].shape, lambda b, pt: (0, 0)),
                  pl.BlockSpec(w["w_uv_flat"].shape, lambda b, pt: (0, 0)),
                  pl.BlockSpec(memory_space=pl.ANY),
                  pl.BlockSpec(memory_space=pl.ANY)],
        out_specs=pl.BlockSpec((None, 1, N_HEADS * D_V), lambda b, pt: (b, 0, 0)),
        scratch_shapes=[pltpu.VMEM((2, chunk_rows, kv_rank), F32),
                        pltpu.VMEM((2, chunk_rows, D_ROPE), F32),
                        pltpu.VMEM((n_k + 16, kv_rank), BF16),
                        pltpu.SemaphoreType.DMA((2,)),
                        pltpu.SemaphoreType.DMA((2,)),
                        pltpu.VMEM((N_HEADS, 1), F32),
                        pltpu.VMEM((N_HEADS, 1), F32),
                        pltpu.VMEM((N_HEADS, kv_rank), F32)])
    return pl.pallas_call(
        kernel,
        out_shape=jax.ShapeDtypeStruct((db, 1, N_HEADS * D_V), F32),
        grid_spec=grid_spec,
        compiler_params=_cparams(("arbitrary",)),
        name="dec_attn",
    )(page_table, q3, qpe3, c_new, kpe_new, w["w_ukt"], w["w_ukq"], w["w_uv_flat"], cache_lat, cache_kpe)


def _merge_tail(x, o, window_sums, u, inv_cnt, sa, sb, gt, wao_ref, wpool_ref, spool_ref, wout_ref):
    a = _dot(o, wao_ref[...])
    group = u.shape[-1] // len(POOL_WINDOWS)
    parts = []
    for g in range(len(POOL_WINDOWS)):
        lo = g * group
        d = window_sums[g] * inv_cnt[g] - u[:, lo:lo + group]
        parts.append(_dot(d.astype(BF16), wpool_ref[g]))
    bb = jnp.concatenate(parts, axis=1) * spool_ref[...]
    m = sa.astype(F32) * a + sb.astype(F32) * bb
    return x + gt * _dot(m.astype(BF16), wout_ref[...])


def _merge_prompt_kernel(x_ref, o_ref, u_ref, halo_ref, sa_ref, sb_ref, gt_ref, wao_ref, wpool_ref, spool_ref,
                         wout_ref, y_ref, ext, *, tm, tiles_per_seq, halo):
    i = pl.program_id(0)
    t_in_seq = i % tiles_per_seq
    u = u_ref[...]
    ext[0:halo, :] = jnp.where(t_in_seq == 0, 0.0, halo_ref[...])
    ext[halo:, :] = u
    group = u.shape[-1] // len(POOL_WINDOWS)
    pos = t_in_seq * tm + lax.broadcasted_iota(jnp.int32, (tm, 1), 0)
    sums, invs = [], []
    for g, wnd in enumerate(POOL_WINDOWS):
        lo = g * group
        s = u[:, lo:lo + group]
        for k in range(1, wnd):
            s = s + ext[halo - k:halo - k + tm, lo:lo + group]
        sums.append(s)
        invs.append(1.0 / jnp.minimum(pos + 1, wnd).astype(F32))
    y_ref[...] = _merge_tail(x_ref[...], o_ref[...], sums, u, invs, sa_ref[...], sb_ref[...], gt_ref[...],
                             wao_ref, wpool_ref, spool_ref, wout_ref)


def _merge_sample_kernel(x_ref, o_ref, u_ref, st_ref, sa_ref, sb_ref, gt_ref, wao_ref, wpool_ref, spool_ref,
                         wout_ref, y_ref):
    u = u_ref[...]
    group = u.shape[-1] // len(POOL_WINDOWS)
    n_state = st_ref.shape[0]
    sums, invs = [], []
    for g, wnd in enumerate(POOL_WINDOWS):
        lo = g * group
        s = u[:, lo:lo + group]
        for k in range(1, wnd):
            s = s + st_ref[n_state - k, :, lo:lo + group]
        sums.append(s)
        invs.append(1.0 / wnd)
    y_ref[...] = _merge_tail(x_ref[...], o_ref[...].astype(BF16), sums, u, invs, sa_ref[...], sb_ref[...],
                             gt_ref[...], wao_ref, wpool_ref, spool_ref, wout_ref)


def _merge_prompt(x, o, u, sa, sb, gt, w, *, seq, tm):
    n, d = x.shape
    d_pool = u.shape[1]
    halo = 2 * SUBLANES
    tiles_per_seq = seq // tm
    row = lambda width: pl.BlockSpec((tm, width), lambda i: (i, 0))
    halo_spec = pl.BlockSpec((halo, d_pool), lambda i: (jnp.maximum(i * (tm // halo) - 1, 0), 0))
    gt_spec = pl.BlockSpec((None, 1, d), lambda i: (i // tiles_per_seq, 0, 0))
    consts = [w["w_attn_o_pad"], w["w_pool"], w["s_pool"], w["w_out"]]
    return pl.pallas_call(
        functools.partial(_merge_prompt_kernel, tm=tm, tiles_per_seq=tiles_per_seq, halo=halo),
        out_shape=jax.ShapeDtypeStruct((n, d), F32),
        grid=(n // tm,),
        in_specs=[row(d), row(o.shape[1]), row(d_pool), halo_spec, row(d), row(d), gt_spec]
                 + [_const_spec(c.shape) for c in consts],
        out_specs=row(d),
        scratch_shapes=[pltpu.VMEM((halo + tm, d_pool), F32)],
        compiler_params=_cparams(("arbitrary",)),
        name="merge_prompt",
    )(x, o, u, u, sa, sb, gt, *consts)


def _merge_sample(x, o, u, state_t, sa, sb, gt, w):
    n, d = x.shape
    consts = [w["w_attn_o_flat"], w["w_pool"], w["s_pool"], w["w_out"]]
    args = [x, o, u, state_t, sa, sb, gt] + consts
    return pl.pallas_call(
        _merge_sample_kernel,
        out_shape=jax.ShapeDtypeStruct((n, d), F32),
        grid=(1,),
        in_specs=[pl.BlockSpec(a.shape, lambda i, nd=a.ndim: (0,) * nd) for a in args],
        out_specs=pl.BlockSpec((n, d), lambda i: (0, 0)),
        compiler_params=_cparams(("arbitrary",)),
        name="merge_sample",
    )(*args)


def _ffn_kernel(x_ref, sh_ref, sc_ref, gt_ref, g2_ref, wg_ref, wu_ref, wd_ref, y_ref):
    x = x_ref[...]
    h = (_rms(x) * g2_ref[...] * (1.0 + sc_ref[...]) + sh_ref[...]).astype(BF16)
    gg = _dot(h, wg_ref[...])
    up = _dot(h, wu_ref[...])
    act = (gg * jax.nn.sigmoid(gg) * up).astype(BF16)
    y_ref[...] = x + gt_ref[...] * _dot(act, wd_ref[...])


def _ffn(x, sh, sc, gt, w, *, rows_per_mod, tm):
    n, d = x.shape
    if rows_per_mod == 1:
        mod_spec = pl.BlockSpec((tm, d), lambda i: (i, 0))
        sh, sc, gt = sh.reshape(n, d), sc.reshape(n, d), gt.reshape(n, d)
    else:
        tiles_per_mod = rows_per_mod // tm
        mod_spec = pl.BlockSpec((None, 1, d), lambda i: (i // tiles_per_mod, 0, 0))
    consts = [w["g_norm2"], w["w_gate"], w["w_up"], w["w_down"]]
    return pl.pallas_call(
        _ffn_kernel,
        out_shape=jax.ShapeDtypeStruct((n, d), F32),
        grid=(n // tm,),
        in_specs=[pl.BlockSpec((tm, d), lambda i: (i, 0)), mod_spec, mod_spec, mod_spec]
                 + [_const_spec(c.shape) for c in consts],
        out_specs=pl.BlockSpec((tm, d), lambda i: (i, 0)),
        compiler_params=_cparams(("arbitrary",)),
        name="ffn",
    )(x, sh, sc, gt, *consts)


def _pad_heads(t, width=HEAD_BLOCK):
    pad = [(0, 0)] * (t.ndim - 1) + [(0, width - t.shape[-1])]
    t = jnp.pad(t, pad)
    return t.reshape(*t.shape[:-2], t.shape[-2] * width)


def _swap_halves(t):
    half = t.shape[-1] // 2
    return jnp.concatenate([t[..., half:], t[..., :half]], axis=-1)


def _rope_block(t):
    pad = [(0, 0)] * (t.ndim - 1) + [(D_NOPE, HEAD_BLOCK - D_NOPE - D_ROPE)]
    return jnp.pad(t, pad)


def _prep_weights(l, w_in, g_norm1, g_q_lat, w_uq, g_kv_lat, g_q_nope, g_q_rope, g_k_nope, g_k_rope, w_uk, w_uv,
                  w_attn_o, w_pool, s_pool, w_out, g_norm2, w_gu, w_down):
    d = w_in.shape[1]
    q_rank = w_uq.shape[1]
    kv_rank = w_uk.shape[1]
    d_pool = w_pool.shape[1] * w_pool.shape[2]
    d_ff = w_down.shape[1]
    win = w_in[l]
    o1 = q_rank
    o2 = o1 + kv_rank
    o3 = o2 + D_ROPE
    o4 = o3 + d_pool
    o5 = o4 + d
    w_kpe = win[:, o2:o3]
    w_in_new = jnp.concatenate(
        [win[:, :o1], win[:, o1:o2], win[:, o3:o4], win[:, o4:o5], win[:, o5:],
         _rope_block(w_kpe), _rope_block(_swap_halves(w_kpe))], axis=1).astype(BF16)

    wuq = w_uq[l]
    zeros_nope = jnp.zeros_like(wuq[..., :D_NOPE])
    wuq_swap = jnp.concatenate([zeros_nope, _swap_halves(wuq[..., D_NOPE:])], axis=-1)
    ones = lambda k: jnp.ones((k,), F32)
    gq = jnp.concatenate([g_q_nope[l], g_q_rope[l]]) * SM_SCALE
    gq_swap = jnp.concatenate([jnp.zeros((D_NOPE,), F32), _swap_halves(g_q_rope[l])]) * SM_SCALE
    tile_heads = lambda v: _pad_heads(jnp.tile(v[None, :], (N_HEADS, 1)))[None, :]

    blk_q = np.zeros((HEAD_BLOCK, HEAD_BLOCK), np.float32)
    blk_q[:D_NOPE, :D_NOPE] = 1.0 / D_NOPE
    blk_q[D_NOPE:D_NOPE + D_ROPE, D_NOPE:D_NOPE + D_ROPE] = 1.0 / D_ROPE
    blk_k = np.zeros((HEAD_BLOCK, HEAD_BLOCK), np.float32)
    blk_k[:D_NOPE, :D_NOPE] = 1.0 / D_NOPE
    eye = np.eye(N_HEADS, dtype=np.float32)

    wuk = w_uk[l]
    wuv = w_uv[l]
    wukt = jnp.transpose(wuk, (1, 2, 0))
    return dict(
        q_rank=q_rank, kv_rank=kv_rank, d_pool=d_pool,
        g_norm1=g_norm1[l][None, :], w_in=w_in_new, g_q_lat=g_q_lat[l][None, :],
        w_uq=_pad_heads(wuq).astype(BF16), w_uq_swap=_pad_heads(wuq_swap).astype(BF16),
        gmat_q=jnp.asarray(np.kron(eye, blk_q), BF16), gq=tile_heads(gq), gq_swap=tile_heads(gq_swap),
        g_kv_lat=g_kv_lat[l][None, :],
        w_uk=_pad_heads(wuk).astype(BF16), gmat_k=jnp.asarray(np.kron(eye, blk_k), BF16),
        gk=tile_heads(g_k_nope[l]), w_uv=_pad_heads(wuv).astype(BF16),
        gkr=_rope_block(g_k_rope[l])[None, :], gkr_swap=_rope_block(_swap_halves(g_k_rope[l]))[None, :],
        w_ukt=wukt.reshape(N_HEADS * D_NOPE, kv_rank).astype(BF16),
        w_ukq=_pad_heads(jnp.transpose(wukt * g_k_nope[l][None, :, None], (2, 0, 1))).T.astype(BF16),
        w_uv_flat=wuv.reshape(kv_rank, N_HEADS * D_V).astype(BF16),
        w_attn_o_pad=jnp.pad(w_attn_o[l], ((0, 0), (0, HEAD_BLOCK - D_V), (0, 0))).reshape(N_HEADS * HEAD_BLOCK, d).astype(BF16),
        w_attn_o_flat=w_attn_o[l].reshape(N_HEADS * D_V, d).astype(BF16),
        w_pool=w_pool[l].astype(BF16), s_pool=s_pool[l][None, :], w_out=w_out[l].astype(BF16),
        g_norm2=g_norm2[l][None, :], w_gate=w_gu[l][:, :d_ff].astype(BF16), w_up=w_gu[l][:, d_ff:].astype(BF16),
        w_down=w_down[l].astype(BF16),
    )


def _rope_tables(pos):
    inv = 1.0 / (ROPE_THETA ** (jnp.arange(0, D_ROPE, 2, dtype=F32) / D_ROPE))
    ang = pos.astype(F32)[:, None] * inv[None, :]
    cos, sin = jnp.cos(ang), jnp.sin(ang)
    n = pos.shape[0]
    cos_t = jnp.concatenate([jnp.ones((n, D_NOPE), F32), cos, cos,
                             jnp.zeros((n, HEAD_BLOCK - D_NOPE - D_ROPE), F32)], axis=1)
    sin_t = jnp.concatenate([jnp.zeros((n, D_NOPE), F32), -sin, sin,
                             jnp.zeros((n, HEAD_BLOCK - D_NOPE - D_ROPE), F32)], axis=1)
    return cos_t, sin_t


def _pick_tile(n, pref):
    t = min(pref, n)
    while n % t:
        t //= 2
    return t


def kernel(x_prompt, x_sample, cache_kv_latent, cache_k_rope, state_pool, page_table, c_prompt, c_sample, w_ada, b_ada, g_norm1, w_in, g_q_lat, w_uq, g_kv_lat, g_q_nope, g_q_rope, g_k_nope, g_k_rope, w_uk, w_uv, w_attn_o, w_pool, s_pool, w_out, g_norm2, w_gu, w_down):
    batch, seq, d = x_prompt.shape
    db, dec_seq, _ = x_sample.shape
    assert dec_seq == 1
    depth = w_in.shape[0]
    n_pages = page_table.shape[1]
    page = cache_kv_latent.shape[2]
    past_len = n_pages * page
    assert past_len >= POOL_STATE
    n_p = batch * seq

    cos_p, sin_p = _rope_tables(jnp.arange(seq))
    cos_p = jnp.tile(cos_p, (batch, 1))
    sin_p = jnp.tile(sin_p, (batch, 1))
    cos_s, sin_s = _rope_tables(jnp.full((db,), past_len))

    n_c = batch + db
    n_c_pad = -(-n_c // SUBLANES) * SUBLANES
    c_all = jnp.pad(jnp.concatenate([c_prompt, c_sample], axis=0), ((0, n_c_pad - n_c), (0, 0)))

    tm_in = _pick_tile(seq, 512)
    tm_ffn = _pick_tile(seq, 256)
    tq = _pick_tile(seq, 512)

    yp = x_prompt.reshape(n_p, d)
    ys = x_sample.reshape(db, d)
    outs = {k: [] for k in ("lat_p", "kr_p", "pool_p", "lat_s", "kr_s", "pool_s")}
    for l in range(depth):
        w = _prep_weights(l, w_in, g_norm1, g_q_lat, w_uq, g_kv_lat, g_q_nope, g_q_rope, g_k_nope, g_k_rope, w_uk,
                          w_uv, w_attn_o, w_pool, s_pool, w_out, g_norm2, w_gu, w_down)
        mod = _adaln(c_all, w_ada[l].astype(BF16), b_ada[l][None, :])
        mod_p = mod[:batch].reshape(batch, 1, 6, d)
        mod_s = mod[batch:n_c].reshape(db, 1, 6, d)
        sh1p, sc1p, gt1p, sh2p, sc2p, gt2p = [mod_p[:, :, j] for j in range(6)]
        sh1s, sc1s, gt1s, sh2s, sc2s, gt2s = [mod_s[:, :, j] for j in range(6)]

        q, k, v, lat, kpe, u, sa, sb = _inproj(yp, sh1p, sc1p, cos_p, sin_p, w, rows_per_mod=seq, tm=tm_in)
        o = _attn(q, k, v, batch=batch, seq=seq, tq=tq)
        x1 = _merge_prompt(yp, o, u, sa, sb, gt1p, w, seq=seq, tm=tm_in)
        yp = _ffn(x1, sh2p, sc2p, gt2p, w, rows_per_mod=seq, tm=tm_ffn)
        outs["lat_p"].append(lat.reshape(batch, seq, -1))
        outs["kr_p"].append(kpe[:, D_NOPE:D_NOPE + D_ROPE].reshape(batch, seq, D_ROPE))
        outs["pool_p"].append(u.reshape(batch, seq, -1)[:, seq - POOL_STATE:])

        qs, _, _, lat_s, kpe_s, u_s, sa_s, sb_s = _inproj(ys, sh1s, sc1s, cos_s, sin_s, w, rows_per_mod=1, tm=db)
        kr_s = kpe_s[:, D_NOPE:D_NOPE + D_ROPE]
        q3 = qs.astype(F32).reshape(db, N_HEADS, HEAD_BLOCK)
        o_s = _dec_attn(page_table, q3, q3[:, :, D_NOPE:D_NOPE + D_ROPE], lat_s[:, None, :], kr_s[:, None, :], w,
                        cache_kv_latent, cache_k_rope, layer=l)
        state_t = jnp.transpose(state_pool[l], (1, 0, 2))
        x1s = _merge_sample(ys, o_s.reshape(db, -1), u_s, state_t, sa_s, sb_s, gt1s.reshape(db, d), w)
        ys = _ffn(x1s, sh2s, sc2s, gt2s, w, rows_per_mod=1, tm=db)
        outs["lat_s"].append(lat_s[:, None, :])
        outs["kr_s"].append(kr_s[:, None, :])
        outs["pool_s"].append(jnp.concatenate([state_pool[l][:, 1:], u_s[:, None, :]], axis=1))

    st = lambda name: jnp.stack(outs[name])
    return (yp.reshape(batch, seq, d), ys.reshape(db, 1, d), st("lat_p"), st("kr_p"), st("pool_p"),
            st("lat_s"), st("kr_s"), st("pool_s"))
```
